```python
import math
import jax, jax.numpy as jnp
from jax import lax
import numpy as np

D_MODEL = 1024
BATCH = 8
SEQ = 2048
DEPTH = 2

CHUNK = 64
PLE_DIM = 256
MIX_WIDTH = D_MODEL // 2
N_BRANCH = 3
A_HEAD_DIM = 64
A_HEADS = MIX_WIDTH // A_HEAD_DIM
A_LEFT_CHUNKS = 8
A_BAND = (A_LEFT_CHUNKS + 1) * CHUNK
REL_CLIP = 128
B_KERNEL = 31
C_HEAD_DIM = 128
C_HEADS = MIX_WIDTH // C_HEAD_DIM
C_CONV = 4
D_FF = 4 * D_MODEL
ALPHA = (2 * DEPTH) ** 0.25
BETA = (8 * DEPTH) ** -0.25
LN_EPS = 1e-5
RMS_EPS = 1e-6
NEG_INF = -1e30
IN_SIZES = ([MIX_WIDTH] * 3
            + [2 * MIX_WIDTH]
            + [MIX_WIDTH] * 4
            + [C_HEADS] * 2
            + [D_MODEL] * N_BRANCH)
IN_WIDTH = sum(IN_SIZES)

kernel_name = "hybrid_streaming_encoder_block"


def _split_points():
    pts, acc = [], 0
    for s in IN_SIZES[:-1]:
        acc += s
        pts.append(acc)
    return pts


def layer_norm(x, g, b):
    xf = x.astype(jnp.float32)
    mu = jnp.mean(xf, axis=-1, keepdims=True)
    var = jnp.mean(jnp.square(xf - mu), axis=-1, keepdims=True)
    y = (xf - mu) * lax.rsqrt(var + LN_EPS) * g.astype(jnp.float32) + b.astype(jnp.float32)
    return y.astype(x.dtype)


def causal_depthwise_conv(x, w):
    k = w.shape[0]
    return lax.conv_general_dilated(
        x, w[:, None, :].astype(x.dtype), window_strides=(1,), padding=[(k - 1, 0)],
        dimension_numbers=("NWC", "WIO", "NWC"), feature_group_count=x.shape[-1])


def chunked_rel_attention(q, k, v, rel_bias):
    bsz, t, _ = q.shape
    nc = t // CHUNK
    shp = (bsz, nc, CHUNK, A_HEADS, A_HEAD_DIM)
    q = q.reshape(shp) * (A_HEAD_DIM ** -0.5)
    k = k.reshape(shp)
    v = v.reshape(shp)
    pad = ((0, 0), (A_LEFT_CHUNKS, 0), (0, 0), (0, 0), (0, 0))
    kp, vp = jnp.pad(k, pad), jnp.pad(v, pad)
    kb = jnp.concatenate([kp[:, j:j + nc] for j in range(A_LEFT_CHUNKS + 1)], axis=2)
    vb = jnp.concatenate([vp[:, j:j + nc] for j in range(A_LEFT_CHUNKS + 1)], axis=2)
    s = jnp.einsum("bnqhd,bnkhd->bnhqk", q, kb).astype(jnp.float32)
    qi = jnp.arange(CHUNK)[:, None]
    ki = jnp.arange(A_BAND)[None, :]
    rel = jnp.clip(qi + A_LEFT_CHUNKS * CHUNK - ki, -REL_CLIP, REL_CLIP) + REL_CLIP
    bias = rel_bias.astype(jnp.float32)[:, rel]
    key_chunk = jnp.arange(nc)[:, None] - A_LEFT_CHUNKS + jnp.arange(A_BAND)[None, :] // CHUNK
    valid = (key_chunk >= 0)[None, :, None, None, :]
    s = jnp.where(valid, s + bias[None, None], NEG_INF)
    pr = jax.nn.softmax(s, axis=-1).astype(v.dtype)
    o = jnp.einsum("bnhqk,bnkhd->bnqhd", pr, vb)
    return o.reshape(bsz, t, A_HEADS * A_HEAD_DIM)


def conformer_conv(u, conv_w, conv_b, ln_g, ln_b):
    a, gate = jnp.split(u, 2, axis=-1)
    hdn = a * jax.nn.sigmoid(gate)
    hdn = causal_depthwise_conv(hdn, conv_w) + conv_b
    hdn = layer_norm(hdn, ln_g, ln_b)
    return jax.nn.silu(hdn)


def l2norm(t):
    return t * lax.rsqrt(jnp.sum(jnp.square(t), axis=-1, keepdims=True) + RMS_EPS)


def chunk_gated_delta_rule(q, k, v, beta, g):
    bsz, t, nh, dk = q.shape
    dv = v.shape[-1]
    nc = t // CHUNK
    ch = lambda a: a.reshape((bsz, nc, CHUNK) + a.shape[2:])
    q, k, v, beta, g = ch(q), ch(k), ch(v), ch(beta), ch(g)
    gc = jnp.cumsum(g, axis=2)
    gch = jnp.swapaxes(gc, 2, 3)
    tril = jnp.tril(jnp.ones((CHUNK, CHUNK), bool))
    strict = jnp.tril(jnp.ones((CHUNK, CHUNK), bool), -1)
    diff = gch[..., :, None] - gch[..., None, :]
    decay = jnp.where(tril, jnp.exp(jnp.where(tril, diff, 0.0)), 0.0)
    kbeta = k * beta[..., None]
    lmat = jnp.where(strict, jnp.einsum("bnihd,bnjhd->bnhij", kbeta, k) * decay, 0.0)
    eye = jnp.eye(CHUNK, dtype=jnp.float32)
    tmat = lax.linalg.triangular_solve(eye + lmat, jnp.broadcast_to(eye, lmat.shape),
                                       left_side=True, lower=True, unit_diagonal=True)
    u = jnp.einsum("bnhij,bnjhd->nbhid", tmat, v * beta[..., None])
    w = jnp.einsum("bnhij,bnjhd->nbhid", tmat, kbeta * jnp.exp(gc)[..., None])
    a_intra = jnp.where(tril, jnp.einsum("bnihd,bnjhd->bnhij", q, k) * decay, 0.0)
    a_intra = jnp.moveaxis(a_intra, 1, 0)
    qs = jnp.transpose(q, (1, 0, 3, 2, 4))
    ks = jnp.transpose(k, (1, 0, 3, 2, 4))
    gs = jnp.transpose(gc, (1, 0, 3, 2))

    def step(state, inp):
        q_c, k_c, u_c, w_c, g_c, a_c = inp
        v_new = u_c - jnp.einsum("bhcd,bhde->bhce", w_c, state)
        o_c = (jnp.einsum("bhcd,bhde->bhce", q_c * jnp.exp(g_c)[..., None], state)
               + jnp.einsum("bhij,bhje->bhie", a_c, v_new))
        g_last = g_c[..., -1]
        k_dec = k_c * jnp.exp(g_last[..., None] - g_c)[..., None]
        state = state * jnp.exp(g_last)[..., None, None] + jnp.einsum("bhcd,bhce->bhde", k_dec, v_new)
        return state, o_c

    s0 = jnp.zeros((bsz, nh, dk, dv), jnp.float32)
    _, o = lax.scan(step, s0, (qs, ks, u, w, gs, a_intra))
    return jnp.transpose(o, (1, 0, 3, 2, 4)).reshape(bsz, t, nh, dv)


def gated_deltanet(q, k, v, z, b_logit, a_logit, conv_w, a_log, dt_bias, norm_g):
    bsz, t, _ = q.shape
    qkv = jax.nn.silu(causal_depthwise_conv(jnp.concatenate([q, k, v], axis=-1), conv_w))
    q, k, v = jnp.split(qkv.astype(jnp.float32), 3, axis=-1)
    hs = (bsz, t, C_HEADS, C_HEAD_DIM)
    q = l2norm(q.reshape(hs)) * (C_HEAD_DIM ** -0.5)
    k = l2norm(k.reshape(hs))
    v = v.reshape(hs)
    beta = jax.nn.sigmoid(b_logit.astype(jnp.float32))
    g = -jnp.exp(a_log.astype(jnp.float32)) * jax.nn.softplus(
        a_logit.astype(jnp.float32) + dt_bias.astype(jnp.float32))
    o = chunk_gated_delta_rule(q, k, v, beta, g)
    o = o * lax.rsqrt(jnp.mean(jnp.square(o), axis=-1, keepdims=True) + RMS_EPS) * norm_g.astype(jnp.float32)
    o = o * jax.nn.silu(z.astype(jnp.float32).reshape(hs))
    return o.reshape(bsz, t, MIX_WIDTH).astype(z.dtype)


def setup_inputs(seed: int = 0) -> dict:
    key = jax.random.key(seed)
    ks = jax.random.split(key, 22)
    f32 = jnp.float32
    L, D = DEPTH, D_MODEL
    nrm = lambda kk, shape, scale: jax.random.normal(kk, shape, f32) * scale
    dt = jnp.exp(jax.random.uniform(ks[10], (L, C_HEADS), f32, math.log(1e-3), math.log(1e-1)))
    return {
        "x": nrm(ks[0], (BATCH, SEQ, D), 1.0),
        "p": nrm(ks[1], (DEPTH, BATCH, SEQ, PLE_DIM), 1.0),
        "w_in": nrm(ks[2], (L, D, IN_WIDTH), D ** -0.5),
        "rel_bias": nrm(ks[3], (L, A_HEADS, 2 * REL_CLIP + 1), 0.1),
        "conv_w": nrm(ks[4], (L, B_KERNEL, MIX_WIDTH), B_KERNEL ** -0.5),
        "conv_bias": nrm(ks[5], (L, MIX_WIDTH), 0.02),
        "conv_ln_g": 1.0 + nrm(ks[6], (L, MIX_WIDTH), 0.02),
        "conv_ln_b": nrm(ks[7], (L, MIX_WIDTH), 0.02),
        "dn_conv_w": nrm(ks[8], (L, C_CONV, 3 * MIX_WIDTH), C_CONV ** -0.5),
        "dn_a_log": jnp.log(jax.random.uniform(ks[9], (L, C_HEADS), f32, 1.0, 16.0)),
        "dn_dt_bias": dt + jnp.log(-jnp.expm1(-dt)),
        "dn_norm_g": 1.0 + nrm(ks[11], (L, C_HEAD_DIM), 0.02),
        "w_branch": nrm(ks[12], (L, N_BRANCH, MIX_WIDTH, D), MIX_WIDTH ** -0.5),
        "w_out": nrm(ks[13], (L, D, D), D ** -0.5 * BETA),
        "ln1_g": 1.0 + nrm(ks[14], (L, D), 0.02),
        "ln1_b": nrm(ks[15], (L, D), 0.02),
        "w_up": nrm(ks[16], (L, D, D_FF), D ** -0.5),
        "w_down": nrm(ks[17], (L, D_FF, D), D_FF ** -0.5 * BETA),
        "w_pe_gate": nrm(ks[18], (L, D, D), D ** -0.5),
        "w_pe_proj": nrm(ks[19], (L, PLE_DIM, D), PLE_DIM ** -0.5 * BETA),
        "ln2_g": 1.0 + nrm(ks[20], (L, D), 0.02),
        "ln2_b": nrm(ks[21], (L, D), 0.02),
    }


def reference(x, p, w_in, rel_bias, conv_w, conv_bias, conv_ln_g, conv_ln_b, dn_conv_w, dn_a_log,
              dn_dt_bias, dn_norm_g, w_branch, w_out, ln1_g, ln1_b, w_up, w_down, w_pe_gate,
              w_pe_proj, ln2_g, ln2_b):
    for i in range(DEPTH):
        h = x @ w_in[i]
        (qa, ka, va, ub, qc, kc, vc, zc, bc, ac, gate_a, gate_b, gate_c) = jnp.split(h, _split_points(), axis=-1)
        ya = chunked_rel_attention(qa, ka, va, rel_bias[i])
        yb = conformer_conv(ub, conv_w[i], conv_bias[i], conv_ln_g[i], conv_ln_b[i])
        yc = gated_deltanet(qc, kc, vc, zc, bc, ac, dn_conv_w[i], dn_a_log[i], dn_dt_bias[i], dn_norm_g[i])
        mix = (jax.nn.sigmoid(gate_a) * (ya @ w_branch[i, 0])
               + jax.nn.sigmoid(gate_b) * (yb @ w_branch[i, 1])
               + jax.nn.sigmoid(gate_c) * (yc @ w_branch[i, 2]))
        x = layer_norm(ALPHA * x + mix @ w_out[i], ln1_g[i], ln1_b[i])
        ff = jnp.square(jax.nn.relu(x @ w_up[i])) @ w_down[i]
        pe = jax.nn.sigmoid(x @ w_pe_gate[i]) * (p[i] @ w_pe_proj[i])
        x = layer_norm(ALPHA * x + ff + pe, ln2_g[i], ln2_b[i])
    return x
```

```python
import functools

import jax
import jax.numpy as jnp
from jax import lax
from jax.experimental import pallas as pl
from jax.experimental.pallas import tpu as pltpu

F32 = jnp.float32
BF16 = jnp.bfloat16

D_MODEL = 1024
CHUNK = 64
PLE_DIM = 256
MIX = D_MODEL // 2
A_HEAD_DIM = 64
A_HEADS = MIX // A_HEAD_DIM
A_LEFT = 8 * CHUNK
REL_CLIP = 128
B_KERNEL = 31
C_HEAD_DIM = 128
C_HEADS = MIX // C_HEAD_DIM
C_CONV = 4
C_STACK = C_HEADS * CHUNK
D_FF = 4 * D_MODEL
DEPTH = 2
ALPHA = (2 * DEPTH) ** 0.25
LN_EPS = 1e-5
RMS_EPS = 1e-6
NEG_INF = -1e30

COL_GATES = 0
COL_QA, COL_KA, COL_VA = 6, 7, 8
COL_UB_A, COL_UB_G = 9, 10
COL_QC, COL_KC, COL_VC, COL_ZC = 11, 12, 13, 14
H_WIDTH = 15 * MIX
BA_PAD = 128

VMEM_LIMIT = 56 * 1024 * 1024


def _sigmoid(x):
    return 1.0 / (1.0 + jnp.exp(-x))


def _softplus(x):
    return jnp.maximum(x, 0.0) + jnp.log1p(jnp.exp(-jnp.abs(x)))


def _layer_norm(z, g, b):
    mu = jnp.mean(z, axis=-1, keepdims=True)
    zc = z - mu
    var = jnp.mean(zc * zc, axis=-1, keepdims=True)
    return zc * lax.rsqrt(var + LN_EPS) * g + b


def _dot(a, b):
    return jnp.dot(a, b, preferred_element_type=F32)


def _dot_nt(a, b):
    return lax.dot_general(a, b, (((1,), (1,)), ((), ())), preferred_element_type=F32)


IN_TM, IN_TN = 1024, 512


def _in_proj_kernel(x_ref, w_ref, wba_ref, h_ref, ba_ref, xb_ref):
    @pl.when(pl.program_id(1) == 0)
    def _():
        xb = x_ref[...].astype(BF16)
        xb_ref[...] = xb
        ba_ref[...] = _dot(xb, wba_ref[...])

    h_ref[...] = _dot(xb_ref[...], w_ref[...]).astype(h_ref.dtype)


def _in_proj(x2, w_main, w_ba):
    n = x2.shape[0]
    return pl.pallas_call(
        _in_proj_kernel,
        grid=(n // IN_TM, H_WIDTH // IN_TN),
        in_specs=[
            pl.BlockSpec((IN_TM, D_MODEL), lambda i, j: (i, 0)),
            pl.BlockSpec((D_MODEL, IN_TN), lambda i, j: (0, j)),
            pl.BlockSpec((D_MODEL, BA_PAD), lambda i, j: (0, 0)),
        ],
        out_specs=[
            pl.BlockSpec((IN_TM, IN_TN), lambda i, j: (i, j)),
            pl.BlockSpec((IN_TM, BA_PAD), lambda i, j: (i, 0)),
        ],
        out_shape=[
            jax.ShapeDtypeStruct((n, H_WIDTH), BF16),
            jax.ShapeDtypeStruct((n, BA_PAD), F32),
        ],
        scratch_shapes=[pltpu.VMEM((IN_TM, D_MODEL), BF16)],
        compiler_params=pltpu.CompilerParams(
            dimension_semantics=("parallel", "arbitrary"), vmem_limit_bytes=VMEM_LIMIT),
        name="in_proj",
    )(x2, w_main, w_ba)


ATT_QB = 256
ATT_KB = A_LEFT + ATT_QB


def _attn_bias_tile(rel_bias):
    r = jnp.arange(ATT_QB)[:, None]
    c = jnp.arange(ATT_KB)[None, :]
    rel = jnp.clip(r - (c - A_LEFT), -REL_CLIP, REL_CLIP) + REL_CLIP
    q_chunk = r // CHUNK
    k_chunk = c // CHUNK - A_LEFT // CHUNK
    band = (k_chunk <= q_chunk) & (k_chunk >= q_chunk - A_LEFT // CHUNK)
    return jnp.where(band[None], rel_bias.astype(F32)[:, rel], NEG_INF)


def _attn_kernel(q_ref, k_ref, v_ref, bias_ref, o_ref, kpad_ref, vpad_ref):
    i = pl.program_id(1)

    @pl.when(i == 0)
    def _():
        zeros = jnp.zeros((A_LEFT, MIX), BF16)
        kpad_ref[0:A_LEFT, :] = zeros
        vpad_ref[0:A_LEFT, :] = zeros
        kpad_ref[A_LEFT:, :] = k_ref[0]
        vpad_ref[A_LEFT:, :] = v_ref[0]

    q0 = pl.multiple_of(i * ATT_QB, ATT_QB)
    col = lax.broadcasted_iota(jnp.int32, (ATT_QB, ATT_KB), 1)
    in_seq = col + q0 >= A_LEFT
    for h in range(A_HEADS):
        sl = slice(h * A_HEAD_DIM, (h + 1) * A_HEAD_DIM)
        qh = q_ref[0, :, sl] * (A_HEAD_DIM ** -0.5)
        kh = kpad_ref[pl.ds(q0, ATT_KB), sl]
        vh = vpad_ref[pl.ds(q0, ATT_KB), sl]
        s = _dot_nt(qh, kh)
        s = jnp.where(in_seq, s + bias_ref[h], NEG_INF)
        m = jnp.max(s, axis=-1, keepdims=True)
        p = jnp.exp(s - m)
        l = jnp.sum(p, axis=-1, keepdims=True)
        o = _dot(p.astype(BF16), vh) / l
        o_ref[0, :, sl] = o.astype(o_ref.dtype)


def _attention(h3, bias_tile):
    bsz, t, _ = h3.shape
    return pl.pallas_call(
        _attn_kernel,
        grid=(bsz, t // ATT_QB),
        in_specs=[
            pl.BlockSpec((1, ATT_QB, MIX), lambda b, i: (b, i, COL_QA)),
            pl.BlockSpec((1, t, MIX), lambda b, i: (b, 0, COL_KA)),
            pl.BlockSpec((1, t, MIX), lambda b, i: (b, 0, COL_VA)),
            pl.BlockSpec((A_HEADS, ATT_QB, ATT_KB), lambda b, i: (0, 0, 0)),
        ],
        out_specs=pl.BlockSpec((1, ATT_QB, MIX), lambda b, i: (b, i, 0)),
        out_shape=jax.ShapeDtypeStruct((bsz, t, MIX), BF16),
        scratch_shapes=[pltpu.VMEM((A_LEFT + t, MIX), BF16), pltpu.VMEM((A_LEFT + t, MIX), BF16)],
        compiler_params=pltpu.CompilerParams(
            dimension_semantics=("parallel", "arbitrary"), vmem_limit_bytes=VMEM_LIMIT),
        name="attention",
    )(h3, h3, h3, bias_tile)


CONV_TB = 256
CONV_HALO = 32
CONV_RS = 32


def _bconv_kernel(a_ref, g_ref, w_ref, cb_ref, lg_ref, lb_ref, o_ref, hbuf_ref):
    @pl.when(pl.program_id(1) == 0)
    def _():
        hbuf_ref[0:CONV_HALO, :] = jnp.zeros((CONV_HALO, MIX), F32)

    hbuf_ref[CONV_HALO:, :] = a_ref[0].astype(F32) * _sigmoid(g_ref[0].astype(F32))
    first = CONV_HALO - (B_KERNEL - 1)
    for r in range(CONV_TB // CONV_RS):
        base = r * CONV_RS + first
        acc = jnp.zeros((CONV_RS, MIX), F32)
        for k in range(B_KERNEL):
            acc = acc + w_ref[k:k + 1, :] * hbuf_ref[base + k:base + k + CONV_RS, :]
        y = _layer_norm(acc + cb_ref[...], lg_ref[...], lb_ref[...])
        o_ref[0, r * CONV_RS:(r + 1) * CONV_RS, :] = (y * _sigmoid(y)).astype(o_ref.dtype)
    hbuf_ref[0:CONV_HALO, :] = hbuf_ref[CONV_TB:CONV_TB + CONV_HALO, :]


def _conformer_conv(h3, conv_w, conv_b, ln_g, ln_b):
    bsz, t, _ = h3.shape
    w_pad = jnp.zeros((CONV_HALO, MIX), F32).at[:B_KERNEL].set(conv_w)
    row = lambda v: v.reshape(1, MIX).astype(F32)
    const = lambda shape: pl.BlockSpec(shape, lambda b, i: (0, 0))
    return pl.pallas_call(
        _bconv_kernel,
        grid=(bsz, t // CONV_TB),
        in_specs=[
            pl.BlockSpec((1, CONV_TB, MIX), lambda b, i: (b, i, COL_UB_A)),
            pl.BlockSpec((1, CONV_TB, MIX), lambda b, i: (b, i, COL_UB_G)),
            const((CONV_HALO, MIX)), const((1, MIX)), const((1, MIX)), const((1, MIX)),
        ],
        out_specs=pl.BlockSpec((1, CONV_TB, MIX), lambda b, i: (b, i, 0)),
        out_shape=jax.ShapeDtypeStruct((bsz, t, MIX), BF16),
        scratch_shapes=[pltpu.VMEM((CONV_HALO + CONV_TB, MIX), F32)],
        compiler_params=pltpu.CompilerParams(
            dimension_semantics=("parallel", "arbitrary"), vmem_limit_bytes=VMEM_LIMIT),
        name="conformer_conv",
    )(h3, h3, w_pad, row(conv_b), row(ln_g), row(ln_b))


DN_HALO = 8


def _stack_heads(x):
    return jnp.concatenate(
        [x[:, h * C_HEAD_DIM:(h + 1) * C_HEAD_DIM] for h in range(C_HEADS)], axis=0)


def _deltanet_kernel(q_ref, k_ref, v_ref, z_ref, ba_ref, at_ref, cw_ref, prow_ref, prow_t_ref,
                     ng_ref, o_ref, xbuf_ref, state_ref):
    @pl.when(pl.program_id(1) == 0)
    def _():
        xbuf_ref[0:DN_HALO, :] = jnp.zeros((DN_HALO, 3 * MIX), F32)
        state_ref[...] = jnp.zeros(state_ref.shape, F32)

    xbuf_ref[DN_HALO:, 0:MIX] = q_ref[0].astype(F32)
    xbuf_ref[DN_HALO:, MIX:2 * MIX] = k_ref[0].astype(F32)
    xbuf_ref[DN_HALO:, 2 * MIX:] = v_ref[0].astype(F32)
    first = DN_HALO - (C_CONV - 1)
    acc = jnp.zeros((CHUNK, 3 * MIX), F32)
    for k in range(C_CONV):
        acc = acc + cw_ref[k:k + 1, :] * xbuf_ref[first + k:first + k + CHUNK, :]
    xbuf_ref[0:DN_HALO, :] = xbuf_ref[CHUNK:CHUNK + DN_HALO, :]
    qkv = acc * _sigmoid(acc)

    def l2n(x):
        return x * lax.rsqrt(jnp.sum(x * x, axis=-1, keepdims=True) + RMS_EPS)

    qs = l2n(_stack_heads(qkv[:, 0:MIX])) * (C_HEAD_DIM ** -0.5)
    ks = l2n(_stack_heads(qkv[:, MIX:2 * MIX]))
    vs = _stack_heads(qkv[:, 2 * MIX:])

    ba = ba_ref[0]
    beta_t = _sigmoid(ba)
    g_t = -jnp.exp(prow_ref[0:1, :]) * _softplus(ba + prow_ref[1:2, :])
    ri = lax.broadcasted_iota(jnp.int32, (CHUNK, CHUNK), 0)
    ci = lax.broadcasted_iota(jnp.int32, (CHUNK, CHUNK), 1)
    tril64 = (ci <= ri).astype(F32)
    gc_t = jnp.dot(tril64, g_t, precision=lax.Precision.HIGHEST, preferred_element_type=F32)
    g_row = -jnp.exp(prow_t_ref[0:1, :]) * _softplus(at_ref[0, 0] + prow_t_ref[1:2, :])
    rr = lax.broadcasted_iota(jnp.int32, (C_STACK, C_STACK), 0)
    cc = lax.broadcasted_iota(jnp.int32, (C_STACK, C_STACK), 1)
    same_head = (rr // CHUNK) == (cc // CHUNK)
    lower = same_head & (cc <= rr)
    strict = same_head & (cc < rr)
    upper = (same_head & (rr <= cc)).astype(F32)
    gc_row = jnp.dot(jnp.broadcast_to(g_row, (8, C_STACK)), upper,
                     precision=lax.Precision.HIGHEST, preferred_element_type=F32)[0:1, :]

    def stack_col(tile, lane0, width):
        return jnp.concatenate(
            [jnp.broadcast_to(tile[:, lane0 + h:lane0 + h + 1], (CHUNK, width)) for h in range(C_HEADS)],
            axis=0)

    beta_b = stack_col(beta_t, 0, C_HEAD_DIM)
    gc_b = stack_col(gc_t, C_HEADS, C_STACK)
    gc_b128 = gc_b[:, 0:C_HEAD_DIM]
    decay = jnp.where(lower, jnp.exp(jnp.where(lower, gc_b - gc_row, 0.0)), 0.0)

    kb = ks * beta_b
    ks_bf = ks.astype(BF16)
    lmat = jnp.where(strict, _dot_nt(kb.astype(BF16), ks_bf) * decay, 0.0)
    eye = (rr == cc).astype(F32)
    tmat = eye - lmat
    pw = lmat
    for _ in range(5):
        pw_bf = pw.astype(BF16)
        pw = _dot(pw_bf, pw_bf)
        tmat = tmat + _dot(tmat.astype(BF16), pw.astype(BF16))
    exp_gc = jnp.exp(gc_b128)
    rhs = jnp.concatenate([vs * beta_b, kb * exp_gc], axis=1)
    uw = _dot(tmat.astype(BF16), rhs.astype(BF16))
    u_s, w_s = uw[:, 0:C_HEAD_DIM], uw[:, C_HEAD_DIM:]
    a_intra = jnp.where(lower, _dot_nt(qs.astype(BF16), ks_bf) * decay, 0.0)
    qg = qs * exp_gc

    v_new, o_inter = [], []
    for h in range(C_HEADS):
        rows = slice(h * CHUNK, (h + 1) * CHUNK)
        s_bf = state_ref[h].astype(BF16)
        lhs = jnp.concatenate([w_s[rows], qg[rows]], axis=0).astype(BF16)
        r = _dot(lhs, s_bf)
        v_new.append(u_s[rows] - r[0:CHUNK])
        o_inter.append(r[CHUNK:])
    v_new = jnp.concatenate(v_new, axis=0)
    v_new_bf = v_new.astype(BF16)
    o_s = jnp.concatenate(o_inter, axis=0) + _dot(a_intra.astype(BF16), v_new_bf)

    ng = ng_ref[...]
    for h in range(C_HEADS):
        rows = slice(h * CHUNK, (h + 1) * CHUNK)
        g_last = gc_b128[h * CHUNK + CHUNK - 1:(h + 1) * CHUNK, :]
        k_dec = ks[rows] * jnp.exp(g_last - gc_b128[rows])
        upd = lax.dot_general(k_dec.astype(BF16), v_new_bf[rows], (((0,), (0,)), ((), ())),
                              preferred_element_type=F32)
        state_ref[h] = state_ref[h] * jnp.exp(g_last) + upd
        o = o_s[rows]
        o = o * lax.rsqrt(jnp.mean(o * o, axis=-1, keepdims=True) + RMS_EPS) * ng
        z = z_ref[0, :, h * C_HEAD_DIM:(h + 1) * C_HEAD_DIM].astype(F32)
        o_ref[0, :, h * C_HEAD_DIM:(h + 1) * C_HEAD_DIM] = (o * (z * _sigmoid(z))).astype(o_ref.dtype)


def _deltanet(h3, ba3, conv_w, a_log, dt_bias, norm_g):
    bsz, t, _ = h3.shape
    nc = t // CHUNK
    a_t = ba3[:, :, C_HEADS:2 * C_HEADS].reshape(bsz, nc, CHUNK, C_HEADS)
    a_t = jnp.swapaxes(a_t, 2, 3).reshape(bsz, nc, 1, C_STACK)
    lane_params = lambda v: jnp.zeros((BA_PAD,), F32).at[C_HEADS:2 * C_HEADS].set(v.astype(F32))
    prow = jnp.stack([lane_params(a_log), lane_params(dt_bias)])
    prow_t = jnp.stack([jnp.repeat(a_log.astype(F32), CHUNK), jnp.repeat(dt_bias.astype(F32), CHUNK)])
    col = lambda c: pl.BlockSpec((1, CHUNK, MIX), lambda b, n: (b, n, c))
    const = lambda shape: pl.BlockSpec(shape, lambda b, n: (0, 0))
    return pl.pallas_call(
        _deltanet_kernel,
        grid=(bsz, nc),
        in_specs=[
            col(COL_QC), col(COL_KC), col(COL_VC), col(COL_ZC),
            pl.BlockSpec((1, CHUNK, BA_PAD), lambda b, n: (b, n, 0)),
            pl.BlockSpec((1, 1, 1, C_STACK), lambda b, n: (b, n, 0, 0)),
            const((C_CONV, 3 * MIX)), const((2, BA_PAD)), const((2, C_STACK)), const((1, C_HEAD_DIM)),
        ],
        out_specs=pl.BlockSpec((1, CHUNK, MIX), lambda b, n: (b, n, 0)),
        out_shape=jax.ShapeDtypeStruct((bsz, t, MIX), BF16),
        scratch_shapes=[
            pltpu.VMEM((DN_HALO + CHUNK, 3 * MIX), F32),
            pltpu.VMEM((C_HEADS, C_HEAD_DIM, C_HEAD_DIM), F32),
        ],
        compiler_params=pltpu.CompilerParams(
            dimension_semantics=("parallel", "arbitrary"), vmem_limit_bytes=VMEM_LIMIT),
        name="deltanet",
    )(h3, h3, h3, h3, ba3, a_t, conv_w.astype(F32), prow, prow_t,
      norm_g.reshape(1, C_HEAD_DIM).astype(F32))


MERGE_TM = 512


def _merge_kernel(x_ref, gates_ref, ya_ref, yb_ref, yc_ref, wb_ref, wo_ref, g_ref, b_ref, o_ref):
    mix = jnp.zeros((MERGE_TM, D_MODEL), F32)
    for br, y_ref in enumerate((ya_ref, yb_ref, yc_ref)):
        gate = gates_ref[:, br * D_MODEL:(br + 1) * D_MODEL].astype(F32)
        mix = mix + _sigmoid(gate) * _dot(y_ref[...], wb_ref[br])
    z = ALPHA * x_ref[...] + _dot(mix.astype(BF16), wo_ref[...])
    o_ref[...] = _layer_norm(z, g_ref[...], b_ref[...])


def _merge(x2, h2, ya, yb, yc, w_branch, w_out, ln_g, ln_b):
    n = x2.shape[0]
    tok = lambda width, c=0: pl.BlockSpec((MERGE_TM, width), lambda i: (i, c))
    return pl.pallas_call(
        _merge_kernel,
        grid=(n // MERGE_TM,),
        in_specs=[
            tok(D_MODEL), tok(3 * D_MODEL, COL_GATES), tok(MIX), tok(MIX), tok(MIX),
            pl.BlockSpec((3, MIX, D_MODEL), lambda i: (0, 0, 0)),
            pl.BlockSpec((D_MODEL, D_MODEL), lambda i: (0, 0)),
            pl.BlockSpec((1, D_MODEL), lambda i: (0, 0)),
            pl.BlockSpec((1, D_MODEL), lambda i: (0, 0)),
        ],
        out_specs=tok(D_MODEL),
        out_shape=jax.ShapeDtypeStruct((n, D_MODEL), F32),
        compiler_params=pltpu.CompilerParams(
            dimension_semantics=("parallel",), vmem_limit_bytes=VMEM_LIMIT),
        name="merge_ln1",
    )(x2, h2, ya, yb, yc, w_branch, w_out, ln_g.reshape(1, D_MODEL), ln_b.reshape(1, D_MODEL))


MLP_TM = 512
MLP_FC = 1024


def _mlp_kernel(x_ref, p_ref, wu_ref, wd_ref, wg_ref, wp_ref, g_ref, b_ref, o_ref):
    x = x_ref[...]
    xb = x.astype(BF16)
    acc = _sigmoid(_dot(xb, wg_ref[...])) * _dot(p_ref[...].astype(BF16), wp_ref[...])
    for c in range(D_FF // MLP_FC):
        hid = jnp.maximum(_dot(xb, wu_ref[:, c * MLP_FC:(c + 1) * MLP_FC]), 0.0)
        acc = acc + _dot((hid * hid).astype(BF16), wd_ref[c * MLP_FC:(c + 1) * MLP_FC, :])
    o_ref[...] = _layer_norm(ALPHA * x + acc, g_ref[...], b_ref[...])


def _mlp(x2, p2, w_up, w_down, w_pg, w_pp, ln_g, ln_b):
    n = x2.shape[0]
    once = lambda shape: pl.BlockSpec(shape, lambda i: (0, 0), pipeline_mode=pl.Buffered(1))
    return pl.pallas_call(
        _mlp_kernel,
        grid=(n // MLP_TM,),
        in_specs=[
            pl.BlockSpec((MLP_TM, D_MODEL), lambda i: (i, 0)),
            pl.BlockSpec((MLP_TM, PLE_DIM), lambda i: (i, 0)),
            once((D_MODEL, D_FF)), once((D_FF, D_MODEL)), once((D_MODEL, D_MODEL)),
            once((PLE_DIM, D_MODEL)), once((1, D_MODEL)), once((1, D_MODEL)),
        ],
        out_specs=pl.BlockSpec((MLP_TM, D_MODEL), lambda i: (i, 0)),
        out_shape=jax.ShapeDtypeStruct((n, D_MODEL), F32),
        compiler_params=pltpu.CompilerParams(
            dimension_semantics=("parallel",), vmem_limit_bytes=VMEM_LIMIT),
        name="mlp_ln2",
    )(x2, p2, w_up, w_down, w_pg, w_pp, ln_g.reshape(1, D_MODEL), ln_b.reshape(1, D_MODEL))


def _split_w_in(w):
    o = 0
    parts = {}
    for name, width in (("a", 3 * MIX), ("ub", 2 * MIX), ("c", 4 * MIX), ("ba", 2 * C_HEADS),
                        ("gates", 3 * D_MODEL)):
        parts[name] = w[:, o:o + width]
        o += width
    w_main = jnp.concatenate([parts["gates"], parts["a"], parts["ub"], parts["c"]], axis=1).astype(BF16)
    w_ba = jnp.pad(parts["ba"], ((0, 0), (0, BA_PAD - 2 * C_HEADS))).astype(BF16)
    return w_main, w_ba


def kernel(x, p, w_in, rel_bias, conv_w, conv_bias, conv_ln_g, conv_ln_b, dn_conv_w, dn_a_log,
           dn_dt_bias, dn_norm_g, w_branch, w_out, ln1_g, ln1_b, w_up, w_down, w_pe_gate,
           w_pe_proj, ln2_g, ln2_b):
    bsz, t, d = x.shape
    n = bsz * t
    x2 = x.reshape(n, d)
    for i in range(DEPTH):
        w_main, w_ba = _split_w_in(w_in[i])
        h2, ba2 = _in_proj(x2, w_main, w_ba)
        h3 = h2.reshape(bsz, t, H_WIDTH)
        ya = _attention(h3, _attn_bias_tile(rel_bias[i]))
        yb = _conformer_conv(h3, conv_w[i], conv_bias[i], conv_ln_g[i], conv_ln_b[i])
        yc = _deltanet(h3, ba2.reshape(bsz, t, BA_PAD), dn_conv_w[i], dn_a_log[i], dn_dt_bias[i],
                       dn_norm_g[i])
        x2 = _merge(x2, h2, ya.reshape(n, MIX), yb.reshape(n, MIX), yc.reshape(n, MIX),
                    w_branch[i].astype(BF16), w_out[i].astype(BF16), ln1_g[i], ln1_b[i])
        x2 = _mlp(x2, p[i].reshape(n, PLE_DIM), w_up[i].astype(BF16), w_down[i].astype(BF16),
                  w_pe_gate[i].astype(BF16), w_pe_proj[i].astype(BF16), ln2_g[i], ln2_b[i])
    return x2.reshape(bsz, t, d)
```

```python
import functools

import jax
import jax.numpy as jnp
from jax import lax
from jax.experimental import pallas as pl
from jax.experimental.pallas import tpu as pltpu

F32 = jnp.float32
BF16 = jnp.bfloat16

D_MODEL = 1024
CHUNK = 64
PLE_DIM = 256
MIX = D_MODEL // 2
A_HEAD_DIM = 64
A_HEADS = MIX // A_HEAD_DIM
A_LEFT = 8 * CHUNK
REL_CLIP = 128
B_KERNEL = 31
C_HEAD_DIM = 128
C_HEADS = MIX // C_HEAD_DIM
C_CONV = 4
C_STACK = C_HEADS * CHUNK
D_FF = 4 * D_MODEL
DEPTH = 2
ALPHA = (2 * DEPTH) ** 0.25
LN_EPS = 1e-5
RMS_EPS = 1e-6
NEG_INF = -1e30

COL_GATES = 0
COL_QA, COL_KA, COL_VA = 6, 7, 8
COL_UB_A, COL_UB_G = 9, 10
COL_QC, COL_KC, COL_VC, COL_ZC = 11, 12, 13, 14
H_WIDTH = 15 * MIX
BA_PAD = 128

LANES, SUBLANES = 128, 8
VMEM_LIMIT = 56 * 1024 * 1024


def _sigmoid(x):
    return 1.0 / (1.0 + jnp.exp(-x))


def _softplus(x):
    return jnp.maximum(x, 0.0) + jnp.log1p(jnp.exp(-jnp.abs(x)))


def _layer_norm(z, g, b):
    mu = jnp.mean(z, axis=-1, keepdims=True)
    zc = z - mu
    var = jnp.mean(zc * zc, axis=-1, keepdims=True)
    return zc * lax.rsqrt(var + LN_EPS) * g + b


def _dot(a, b):
    return jnp.dot(a, b, preferred_element_type=F32)


def _dot_nt(a, b):
    return lax.dot_general(a, b, (((1,), (1,)), ((), ())), preferred_element_type=F32)


IN_TM, IN_TN = 1024, 2560
IN_NC = 1280


def _in_proj_kernel(x_ref, w_ref, wba_ref, h_ref, ba_ref, xb_ref):
    @pl.when(pl.program_id(1) == 0)
    def _():
        xb = x_ref[...].astype(BF16)
        xb_ref[...] = xb
        ba_ref[...] = _dot(xb, wba_ref[...])

    for c in range(IN_TN // IN_NC):
        cols = slice(c * IN_NC, (c + 1) * IN_NC)
        h_ref[:, cols] = _dot(xb_ref[...], w_ref[:, cols]).astype(h_ref.dtype)


def _in_proj(x2, w_main, w_ba):
    n = x2.shape[0]
    return pl.pallas_call(
        _in_proj_kernel,
        grid=(n // IN_TM, H_WIDTH // IN_TN),
        in_specs=[
            pl.BlockSpec((IN_TM, D_MODEL), lambda i, j: (i, 0)),
            pl.BlockSpec((D_MODEL, IN_TN), lambda i, j: (0, j)),
            pl.BlockSpec((D_MODEL, BA_PAD), lambda i, j: (0, 0)),
        ],
        out_specs=[
            pl.BlockSpec((IN_TM, IN_TN), lambda i, j: (i, j)),
            pl.BlockSpec((IN_TM, BA_PAD), lambda i, j: (i, 0)),
        ],
        out_shape=[
            jax.ShapeDtypeStruct((n, H_WIDTH), BF16),
            jax.ShapeDtypeStruct((n, BA_PAD), F32),
        ],
        scratch_shapes=[pltpu.VMEM((IN_TM, D_MODEL), BF16)],
        compiler_params=pltpu.CompilerParams(
            dimension_semantics=("parallel", "arbitrary"), vmem_limit_bytes=VMEM_LIMIT),
        name="in_proj",
    )(x2, w_main, w_ba)


ATT_QB = 256
ATT_KB = A_LEFT + ATT_QB


ATT_BASE = 1024


def _attn_bias_base(rel_bias):
    tab = rel_bias.astype(F32)
    far = jnp.broadcast_to(tab[:, 2 * REL_CLIP:], (A_HEADS, A_LEFT - REL_CLIP + 1))
    near = tab[:, 2 * REL_CLIP - 1:0:-1]
    n_wrapped = ATT_QB - 1
    n_unused = ATT_BASE - far.shape[1] - near.shape[1] - n_wrapped
    unused = jnp.broadcast_to(tab[:, 0:1], (A_HEADS, n_unused))
    wrapped = jnp.broadcast_to(tab[:, 2 * REL_CLIP:], (A_HEADS, n_wrapped))
    return jnp.concatenate([far, near, unused, wrapped], axis=1)


def _attn_kernel(q_ref, k_ref, v_ref, base_ref, o_ref, kpad_ref, vpad_ref, bias_ref):
    i = pl.program_id(1)

    @pl.when((pl.program_id(0) == 0) & (i == 0))
    def _():
        r = lax.broadcasted_iota(jnp.int32, (ATT_QB, ATT_KB), 0)
        c = lax.broadcasted_iota(jnp.int32, (ATT_QB, ATT_KB), 1)
        q_chunk = r // CHUNK
        k_chunk = c // CHUNK - A_LEFT // CHUNK
        band = (k_chunk <= q_chunk) & (k_chunk >= q_chunk - A_LEFT // CHUNK)
        for h in range(A_HEADS):
            rows = jnp.broadcast_to(base_ref[h:h + 1, :], (ATT_QB, ATT_BASE))
            skew = pltpu.roll(rows, 0, 1, stride=1, stride_axis=0)
            bias_ref[h] = jnp.where(band, skew[:, 0:ATT_KB], NEG_INF)

    @pl.when(i == 0)
    def _():
        zeros = jnp.zeros((A_LEFT, MIX), BF16)
        kpad_ref[0:A_LEFT, :] = zeros
        vpad_ref[0:A_LEFT, :] = zeros
        kpad_ref[A_LEFT:, :] = k_ref[0]
        vpad_ref[A_LEFT:, :] = v_ref[0]

    q0 = pl.multiple_of(i * ATT_QB, ATT_QB)
    col = lax.broadcasted_iota(jnp.int32, (ATT_QB, ATT_KB), 1)
    in_seq = col + q0 >= A_LEFT
    for h in range(A_HEADS):
        sl = slice(h * A_HEAD_DIM, (h + 1) * A_HEAD_DIM)
        qh = q_ref[0, :, sl] * (A_HEAD_DIM ** -0.5)
        kh = kpad_ref[pl.ds(q0, ATT_KB), sl]
        vh = vpad_ref[pl.ds(q0, ATT_KB), sl]
        s = _dot_nt(qh, kh)
        s = jnp.where(in_seq, s + bias_ref[h], NEG_INF)
        m = jnp.max(s, axis=-1, keepdims=True)
        p = jnp.exp(s - m)
        l = jnp.sum(p, axis=-1, keepdims=True)
        o = _dot(p.astype(BF16), vh) / l
        o_ref[0, :, sl] = o.astype(o_ref.dtype)


def _attention(h3, rel_bias):
    bsz, t, _ = h3.shape
    return pl.pallas_call(
        _attn_kernel,
        grid=(bsz, t // ATT_QB),
        in_specs=[
            pl.BlockSpec((1, ATT_QB, MIX), lambda b, i: (b, i, COL_QA)),
            pl.BlockSpec((1, t, MIX), lambda b, i: (b, 0, COL_KA)),
            pl.BlockSpec((1, t, MIX), lambda b, i: (b, 0, COL_VA)),
            pl.BlockSpec((A_HEADS, ATT_BASE), lambda b, i: (0, 0)),
        ],
        out_specs=pl.BlockSpec((1, ATT_QB, MIX), lambda b, i: (b, i, 0)),
        out_shape=jax.ShapeDtypeStruct((bsz, t, MIX), BF16),
        scratch_shapes=[
            pltpu.VMEM((A_LEFT + t, MIX), BF16),
            pltpu.VMEM((A_LEFT + t, MIX), BF16),
            pltpu.VMEM((A_HEADS, ATT_QB, ATT_KB), F32),
        ],
        compiler_params=pltpu.CompilerParams(
            dimension_semantics=("arbitrary", "arbitrary"), vmem_limit_bytes=VMEM_LIMIT),
        name="attention",
    )(h3, h3, h3, _attn_bias_base(rel_bias))


CONV_TB = 256
CONV_HALO = 32
CONV_RS = 32


def _bconv_kernel(a_ref, g_ref, w_ref, cb_ref, lg_ref, lb_ref, o_ref, hbuf_ref):
    @pl.when(pl.program_id(1) == 0)
    def _():
        hbuf_ref[0:CONV_HALO, :] = jnp.zeros((CONV_HALO, MIX), F32)

    hbuf_ref[CONV_HALO:, :] = a_ref[0].astype(F32) * _sigmoid(g_ref[0].astype(F32))
    first = CONV_HALO - (B_KERNEL - 1)
    win = CONV_RS + CONV_HALO
    for r in range(CONV_TB // CONV_RS):
        accs = []
        for lt in range(MIX // LANES):
            lanes = slice(lt * LANES, (lt + 1) * LANES)
            x = hbuf_ref[r * CONV_RS:r * CONV_RS + win, lanes]
            acc = jnp.zeros((CONV_RS, LANES), F32)
            for s in range(SUBLANES):
                xs = x if s == 0 else pltpu.roll(x, win - s, 0)
                for m in range(win // SUBLANES):
                    k = SUBLANES * m + s - first
                    if 0 <= k < B_KERNEL and SUBLANES * m + s + CONV_RS <= win:
                        acc = acc + w_ref[k:k + 1, lanes] * xs[SUBLANES * m:SUBLANES * m + CONV_RS]
            accs.append(acc)
        acc = jnp.concatenate(accs, axis=1)
        y = _layer_norm(acc + cb_ref[...], lg_ref[...], lb_ref[...])
        o_ref[0, r * CONV_RS:(r + 1) * CONV_RS, :] = (y * _sigmoid(y)).astype(o_ref.dtype)
    hbuf_ref[0:CONV_HALO, :] = hbuf_ref[CONV_TB:CONV_TB + CONV_HALO, :]


def _conformer_conv(h3, conv_w, conv_b, ln_g, ln_b):
    bsz, t, _ = h3.shape
    w_pad = jnp.zeros((CONV_HALO, MIX), F32).at[:B_KERNEL].set(conv_w)
    row = lambda v: v.reshape(1, MIX).astype(F32)
    const = lambda shape: pl.BlockSpec(shape, lambda b, i: (0, 0))
    return pl.pallas_call(
        _bconv_kernel,
        grid=(bsz, t // CONV_TB),
        in_specs=[
            pl.BlockSpec((1, CONV_TB, MIX), lambda b, i: (b, i, COL_UB_A)),
            pl.BlockSpec((1, CONV_TB, MIX), lambda b, i: (b, i, COL_UB_G)),
            const((CONV_HALO, MIX)), const((1, MIX)), const((1, MIX)), const((1, MIX)),
        ],
        out_specs=pl.BlockSpec((1, CONV_TB, MIX), lambda b, i: (b, i, 0)),
        out_shape=jax.ShapeDtypeStruct((bsz, t, MIX), BF16),
        scratch_shapes=[pltpu.VMEM((CONV_HALO + CONV_TB, MIX), F32)],
        compiler_params=pltpu.CompilerParams(
            dimension_semantics=("parallel", "arbitrary"), vmem_limit_bytes=VMEM_LIMIT),
        name="conformer_conv",
    )(h3, h3, w_pad, row(conv_b), row(ln_g), row(ln_b))


DN_HALO = 8
DN_CPS = 4
DN_ROWS = DN_CPS * CHUNK
assert DN_ROWS == C_STACK


def _stack_heads(x):
    return jnp.concatenate(
        [x[:, h * C_HEAD_DIM:(h + 1) * C_HEAD_DIM] for h in range(C_HEADS)], axis=0)


def _deltanet_kernel(q_ref, k_ref, v_ref, z_ref, ba_ref, at_ref, cw_ref, prow_ref, prow_t_ref,
                     ng_ref, o_ref, xbuf_ref, state_ref):
    @pl.when(pl.program_id(1) == 0)
    def _():
        xbuf_ref[0:DN_HALO, :] = jnp.zeros((DN_HALO, 3 * MIX), F32)
        state_ref[...] = jnp.zeros(state_ref.shape, F32)

    xbuf_ref[DN_HALO:, 0:MIX] = q_ref[0].astype(F32)
    xbuf_ref[DN_HALO:, MIX:2 * MIX] = k_ref[0].astype(F32)
    xbuf_ref[DN_HALO:, 2 * MIX:] = v_ref[0].astype(F32)
    first = DN_HALO - (C_CONV - 1)

    rr = lax.broadcasted_iota(jnp.int32, (C_STACK, C_STACK), 0)
    cc = lax.broadcasted_iota(jnp.int32, (C_STACK, C_STACK), 1)
    same_block = (rr // CHUNK) == (cc // CHUNK)
    lower = same_block & (cc <= rr)
    strict = same_block & (cc < rr)
    lower_f = lower.astype(F32)
    upper_f = (same_block & (rr <= cc)).astype(F32)
    eye = (rr == cc).astype(F32)

    ba = ba_ref[0]
    beta_all = _sigmoid(ba)
    g_all = -jnp.exp(prow_ref[0:1, :]) * _softplus(ba + prow_ref[1:2, :])
    gc_all = jnp.dot(lower_f, g_all, precision=lax.Precision.HIGHEST, preferred_element_type=F32)
    g_rows = -jnp.exp(prow_t_ref[0:1, :]) * _softplus(at_ref[0, 0] + prow_t_ref[1:2, :])
    g_rows = jnp.concatenate([g_rows, jnp.zeros((SUBLANES - DN_CPS, C_STACK), F32)], axis=0)
    gc_rows = jnp.dot(g_rows, upper_f, precision=lax.Precision.HIGHEST, preferred_element_type=F32)

    def l2n(x):
        return x * lax.rsqrt(jnp.sum(x * x, axis=-1, keepdims=True) + RMS_EPS)

    def stack_col(tile, lane0, width):
        return jnp.concatenate(
            [jnp.broadcast_to(tile[:, lane0 + h:lane0 + h + 1], (CHUNK, width)) for h in range(C_HEADS)],
            axis=0)

    chunks = range(DN_CPS)
    qs, ks, vs, ks_bf = [], [], [], []
    for c in chunks:
        acc = jnp.zeros((CHUNK, 3 * MIX), F32)
        for k in range(C_CONV):
            acc = acc + cw_ref[k:k + 1, :] * xbuf_ref[c * CHUNK + first + k:(c + 1) * CHUNK + first + k, :]
        qkv = acc * _sigmoid(acc)
        qs.append(l2n(_stack_heads(qkv[:, 0:MIX])) * (C_HEAD_DIM ** -0.5))
        ks.append(l2n(_stack_heads(qkv[:, MIX:2 * MIX])))
        vs.append(_stack_heads(qkv[:, 2 * MIX:]))
        ks_bf.append(ks[c].astype(BF16))
    xbuf_ref[0:DN_HALO, :] = xbuf_ref[DN_ROWS:DN_ROWS + DN_HALO, :]

    beta_b, gc_b128, decay, kb = [], [], [], []
    for c in chunks:
        tok = slice(c * CHUNK, (c + 1) * CHUNK)
        beta_b.append(stack_col(beta_all[tok], 0, C_HEAD_DIM))
        gc_b = stack_col(gc_all[tok], C_HEADS, C_STACK)
        gc_b128.append(gc_b[:, 0:C_HEAD_DIM])
        decay.append(jnp.where(lower, jnp.exp(jnp.where(lower, gc_b - gc_rows[c:c + 1, :], 0.0)), 0.0))
        kb.append(ks[c] * beta_b[c])
    lmat = [jnp.where(strict, _dot_nt(kb[c].astype(BF16), ks_bf[c]) * decay[c], 0.0) for c in chunks]
    tmat = [eye - lmat[c] for c in chunks]
    pw = lmat
    for _ in range(5):
        pw_bf = [pw[c].astype(BF16) for c in chunks]
        pw = [_dot(pw_bf[c], pw_bf[c]) for c in chunks]
        tmat = [tmat[c] + _dot(tmat[c].astype(BF16), pw[c].astype(BF16)) for c in chunks]
    exp_gc = [jnp.exp(gc_b128[c]) for c in chunks]
    uw = [_dot(tmat[c].astype(BF16),
               jnp.concatenate([vs[c] * beta_b[c], kb[c] * exp_gc[c]], axis=1).astype(BF16)) for c in chunks]
    a_intra = [jnp.where(lower, _dot_nt(qs[c].astype(BF16), ks_bf[c]) * decay[c], 0.0).astype(BF16)
               for c in chunks]
    local = [dict(u=uw[c][:, 0:C_HEAD_DIM], w=uw[c][:, C_HEAD_DIM:], qg=qs[c] * exp_gc[c], ks=ks[c],
                  gc=gc_b128[c], a=a_intra[c]) for c in chunks]

    ng = ng_ref[...]
    state = [state_ref[h] for h in range(C_HEADS)]
    for c, lc in enumerate(local):
        v_new, o_inter = [], []
        for h in range(C_HEADS):
            rows = slice(h * CHUNK, (h + 1) * CHUNK)
            lhs = jnp.concatenate([lc["w"][rows], lc["qg"][rows]], axis=0).astype(BF16)
            r = _dot(lhs, state[h].astype(BF16))
            v_new.append(lc["u"][rows] - r[0:CHUNK])
            o_inter.append(r[CHUNK:])
        v_new_bf = jnp.concatenate(v_new, axis=0).astype(BF16)
        o_s = jnp.concatenate(o_inter, axis=0) + _dot(lc["a"], v_new_bf)
        for h in range(C_HEADS):
            rows = slice(h * CHUNK, (h + 1) * CHUNK)
            cols = slice(h * C_HEAD_DIM, (h + 1) * C_HEAD_DIM)
            g_last = lc["gc"][(h + 1) * CHUNK - 1:(h + 1) * CHUNK, :]
            k_dec = lc["ks"][rows] * jnp.exp(g_last - lc["gc"][rows])
            upd = lax.dot_general(k_dec.astype(BF16), v_new_bf[rows], (((0,), (0,)), ((), ())),
                                  preferred_element_type=F32)
            state[h] = state[h] * jnp.exp(g_last) + upd
            o = o_s[rows]
            o = o * lax.rsqrt(jnp.mean(o * o, axis=-1, keepdims=True) + RMS_EPS) * ng
            z = z_ref[0, c * CHUNK:(c + 1) * CHUNK, cols].astype(F32)
            o_ref[0, c * CHUNK:(c + 1) * CHUNK, cols] = (o * (z * _sigmoid(z))).astype(o_ref.dtype)
    for h in range(C_HEADS):
        state_ref[h] = state[h]


def _deltanet(h3, ba3, conv_w, a_log, dt_bias, norm_g):
    bsz, t, _ = h3.shape
    nc = t // CHUNK
    a_t = ba3[:, :, C_HEADS:2 * C_HEADS].reshape(bsz, nc, CHUNK, C_HEADS)
    a_t = jnp.swapaxes(a_t, 2, 3).reshape(bsz, nc // DN_CPS, DN_CPS, C_STACK)
    lane_params = lambda v: jnp.zeros((BA_PAD,), F32).at[C_HEADS:2 * C_HEADS].set(v.astype(F32))
    prow = jnp.stack([lane_params(a_log), lane_params(dt_bias)])
    prow_t = jnp.stack([jnp.repeat(a_log.astype(F32), CHUNK), jnp.repeat(dt_bias.astype(F32), CHUNK)])
    col = lambda c: pl.BlockSpec((1, DN_ROWS, MIX), lambda b, n: (b, n, c))
    const = lambda shape: pl.BlockSpec(shape, lambda b, n: (0, 0))
    return pl.pallas_call(
        _deltanet_kernel,
        grid=(bsz, nc // DN_CPS),
        in_specs=[
            col(COL_QC), col(COL_KC), col(COL_VC), col(COL_ZC),
            pl.BlockSpec((1, DN_ROWS, BA_PAD), lambda b, n: (b, n, 0)),
            pl.BlockSpec((1, 1, DN_CPS, C_STACK), lambda b, n: (b, n, 0, 0)),
            const((C_CONV, 3 * MIX)), const((2, BA_PAD)), const((2, C_STACK)), const((1, C_HEAD_DIM)),
        ],
        out_specs=pl.BlockSpec((1, DN_ROWS, MIX), lambda b, n: (b, n, 0)),
        out_shape=jax.ShapeDtypeStruct((bsz, t, MIX), BF16),
        scratch_shapes=[
            pltpu.VMEM((DN_HALO + DN_ROWS, 3 * MIX), F32),
            pltpu.VMEM((C_HEADS, C_HEAD_DIM, C_HEAD_DIM), F32),
        ],
        compiler_params=pltpu.CompilerParams(
            dimension_semantics=("parallel", "arbitrary"), vmem_limit_bytes=VMEM_LIMIT),
        name="deltanet",
    )(h3, h3, h3, h3, ba3, a_t, conv_w.astype(F32), prow, prow_t,
      norm_g.reshape(1, C_HEAD_DIM).astype(F32))


MERGE_TM = 512


def _merge_kernel(x_ref, gates_ref, ya_ref, yb_ref, yc_ref, wb_ref, wo_ref, g_ref, b_ref, o_ref):
    mix = jnp.zeros((MERGE_TM, D_MODEL), F32)
    for br, y_ref in enumerate((ya_ref, yb_ref, yc_ref)):
        gate = gates_ref[:, br * D_MODEL:(br + 1) * D_MODEL].astype(F32)
        mix = mix + _sigmoid(gate) * _dot(y_ref[...], wb_ref[br])
    z = ALPHA * x_ref[...] + _dot(mix.astype(BF16), wo_ref[...])
    o_ref[...] = _layer_norm(z, g_ref[...], b_ref[...])


def _merge(x2, h2, ya, yb, yc, w_branch, w_out, ln_g, ln_b):
    n = x2.shape[0]
    tok = lambda width, c=0: pl.BlockSpec((MERGE_TM, width), lambda i: (i, c))
    return pl.pallas_call(
        _merge_kernel,
        grid=(n // MERGE_TM,),
        in_specs=[
            tok(D_MODEL), tok(3 * D_MODEL, COL_GATES), tok(MIX), tok(MIX), tok(MIX),
            pl.BlockSpec((3, MIX, D_MODEL), lambda i: (0, 0, 0)),
            pl.BlockSpec((D_MODEL, D_MODEL), lambda i: (0, 0)),
            pl.BlockSpec((1, D_MODEL), lambda i: (0, 0)),
            pl.BlockSpec((1, D_MODEL), lambda i: (0, 0)),
        ],
        out_specs=tok(D_MODEL),
        out_shape=jax.ShapeDtypeStruct((n, D_MODEL), F32),
        compiler_params=pltpu.CompilerParams(
            dimension_semantics=("parallel",), vmem_limit_bytes=VMEM_LIMIT),
        name="merge_ln1",
    )(x2, h2, ya, yb, yc, w_branch, w_out, ln_g.reshape(1, D_MODEL), ln_b.reshape(1, D_MODEL))


MLP_TM = 512
MLP_FC = 1024


def _mlp_kernel(x_ref, p_ref, wu_ref, wd_ref, wg_ref, wp_ref, g_ref, b_ref, o_ref):
    x = x_ref[...]
    xb = x.astype(BF16)
    acc = _sigmoid(_dot(xb, wg_ref[...])) * _dot(p_ref[...].astype(BF16), wp_ref[...])
    for c in range(D_FF // MLP_FC):
        hid = jnp.maximum(_dot(xb, wu_ref[:, c * MLP_FC:(c + 1) * MLP_FC]), 0.0)
        acc = acc + _dot((hid * hid).astype(BF16), wd_ref[c * MLP_FC:(c + 1) * MLP_FC, :])
    o_ref[...] = _layer_norm(ALPHA * x + acc, g_ref[...], b_ref[...])


def _mlp(x2, p2, w_up, w_down, w_pg, w_pp, ln_g, ln_b):
    n = x2.shape[0]
    once = lambda shape: pl.BlockSpec(shape, lambda i: (0, 0), pipeline_mode=pl.Buffered(1))
    return pl.pallas_call(
        _mlp_kernel,
        grid=(n // MLP_TM,),
        in_specs=[
            pl.BlockSpec((MLP_TM, D_MODEL), lambda i: (i, 0)),
            pl.BlockSpec((MLP_TM, PLE_DIM), lambda i: (i, 0)),
            once((D_MODEL, D_FF)), once((D_FF, D_MODEL)), once((D_MODEL, D_MODEL)),
            once((PLE_DIM, D_MODEL)), once((1, D_MODEL)), once((1, D_MODEL)),
        ],
        out_specs=pl.BlockSpec((MLP_TM, D_MODEL), lambda i: (i, 0)),
        out_shape=jax.ShapeDtypeStruct((n, D_MODEL), F32),
        compiler_params=pltpu.CompilerParams(
            dimension_semantics=("parallel",), vmem_limit_bytes=VMEM_LIMIT),
        name="mlp_ln2",
    )(x2, p2, w_up, w_down, w_pg, w_pp, ln_g.reshape(1, D_MODEL), ln_b.reshape(1, D_MODEL))


def _split_w_in(w):
    o = 0
    parts = {}
    for name, width in (("a", 3 * MIX), ("ub", 2 * MIX), ("c", 4 * MIX), ("ba", 2 * C_HEADS),
                        ("gates", 3 * D_MODEL)):
        parts[name] = w[:, o:o + width]
        o += width
    w_main = jnp.concatenate([parts["gates"], parts["a"], parts["ub"], parts["c"]], axis=1).astype(BF16)
    w_ba = jnp.pad(parts["ba"], ((0, 0), (0, BA_PAD - 2 * C_HEADS))).astype(BF16)
    return w_main, w_ba


def kernel(x, p, w_in, rel_bias, conv_w, conv_bias, conv_ln_g, conv_ln_b, dn_conv_w, dn_a_log,
           dn_dt_bias, dn_norm_g, w_branch, w_out, ln1_g, ln1_b, w_up, w_down, w_pe_gate,
           w_pe_proj, ln2_g, ln2_b):
    bsz, t, d = x.shape
    n = bsz * t
    x2 = x.reshape(n, d)
    for i in range(DEPTH):
        w_main, w_ba = _split_w_in(w_in[i])
        h2, ba2 = _in_proj(x2, w_main, w_ba)
        h3 = h2.reshape(bsz, t, H_WIDTH)
        ya = _attention(h3, rel_bias[i])
        yb = _conformer_conv(h3, conv_w[i], conv_bias[i], conv_ln_g[i], conv_ln_b[i])
        yc = _deltanet(h3, ba2.reshape(bsz, t, BA_PAD), dn_conv_w[i], dn_a_log[i], dn_dt_bias[i],
                       dn_norm_g[i])
        x2 = _merge(x2, h2, ya.reshape(n, MIX), yb.reshape(n, MIX), yc.reshape(n, MIX),
                    w_branch[i].astype(BF16), w_out[i].astype(BF16), ln1_g[i], ln1_b[i])
        x2 = _mlp(x2, p[i].reshape(n, PLE_DIM), w_up[i].astype(BF16), w_down[i].astype(BF16),
                  w_pe_gate[i].astype(BF16), w_pe_proj[i].astype(BF16), ln2_g[i], ln2_b[i])
    return x2.reshape(bsz, t, d)
```

```python
import functools

import jax
import jax.numpy as jnp
from jax import lax
from jax.experimental import pallas as pl
from jax.experimental.pallas import tpu as pltpu

F32 = jnp.float32
BF16 = jnp.bfloat16

D_MODEL = 1024
CHUNK = 64
PLE_DIM = 256
MIX = D_MODEL // 2
A_HEAD_DIM = 64
A_HEADS = MIX // A_HEAD_DIM
A_LEFT = 8 * CHUNK
REL_CLIP = 128
B_KERNEL = 31
C_HEAD_DIM = 128
C_HEADS = MIX // C_HEAD_DIM
C_CONV = 4
C_STACK = C_HEADS * CHUNK
D_FF = 4 * D_MODEL
DEPTH = 2
ALPHA = (2 * DEPTH) ** 0.25
LN_EPS = 1e-5
RMS_EPS = 1e-6
NEG_INF = -1e30

COL_GATES = 0
COL_QA, COL_KA, COL_VA = 6, 7, 8
COL_UB_A, COL_UB_G = 9, 10
COL_QC, COL_KC, COL_VC, COL_ZC = 11, 12, 13, 14
H_WIDTH = 15 * MIX
BA_PAD = 128

LANES, SUBLANES = 128, 8
VMEM_LIMIT = 56 * 1024 * 1024


def _sigmoid(x):
    return 1.0 / (1.0 + jnp.exp(-x))


def _softplus(x):
    return jnp.maximum(x, 0.0) + jnp.log1p(jnp.exp(-jnp.abs(x)))


def _layer_norm(z, g, b):
    mu = jnp.mean(z, axis=-1, keepdims=True)
    zc = z - mu
    var = jnp.mean(zc * zc, axis=-1, keepdims=True)
    return zc * lax.rsqrt(var + LN_EPS) * g + b


def _dot(a, b):
    return jnp.dot(a, b, preferred_element_type=F32)


def _dot_nt(a, b):
    return lax.dot_general(a, b, (((1,), (1,)), ((), ())), preferred_element_type=F32)


IN_TM, IN_TN = 1024, 2560
IN_NC = 1280


def _in_proj_kernel(x_ref, w_ref, wba_ref, h_ref, ba_ref, xb_ref):
    @pl.when(pl.program_id(1) == 0)
    def _():
        xb = x_ref[...].astype(BF16)
        xb_ref[...] = xb
        ba_ref[...] = _dot(xb, wba_ref[...])

    for c in range(IN_TN // IN_NC):
        cols = slice(c * IN_NC, (c + 1) * IN_NC)
        h_ref[:, cols] = _dot(xb_ref[...], w_ref[:, cols]).astype(h_ref.dtype)


def _in_proj(x2, w_main, w_ba, layer):
    n = x2.shape[0]
    return pl.pallas_call(
        _in_proj_kernel,
        grid=(n // IN_TM, H_WIDTH // IN_TN),
        in_specs=[
            pl.BlockSpec((IN_TM, D_MODEL), lambda i, j: (i, 0)),
            pl.BlockSpec((None, D_MODEL, IN_TN), lambda i, j: (layer, 0, j)),
            pl.BlockSpec((None, D_MODEL, BA_PAD), lambda i, j: (layer, 0, 0)),
        ],
        out_specs=[
            pl.BlockSpec((IN_TM, IN_TN), lambda i, j: (i, j)),
            pl.BlockSpec((IN_TM, BA_PAD), lambda i, j: (i, 0)),
        ],
        out_shape=[
            jax.ShapeDtypeStruct((n, H_WIDTH), BF16),
            jax.ShapeDtypeStruct((n, BA_PAD), F32),
        ],
        scratch_shapes=[pltpu.VMEM((IN_TM, D_MODEL), BF16)],
        compiler_params=pltpu.CompilerParams(
            dimension_semantics=("parallel", "arbitrary"), vmem_limit_bytes=VMEM_LIMIT),
        name="in_proj",
    )(x2, w_main, w_ba)


ATT_QB = 256
ATT_KB = A_LEFT + ATT_QB


ATT_BASE = 1024
ATT_BIAS_VARIANTS = A_LEFT // ATT_QB + 1


def _attn_bias_base(rel_bias):
    tab = rel_bias.astype(F32)
    far = jnp.broadcast_to(tab[:, 2 * REL_CLIP:], (A_HEADS, A_LEFT - REL_CLIP + 1))
    near = tab[:, 2 * REL_CLIP - 1:0:-1]
    n_wrapped = ATT_QB - 1
    n_unused = ATT_BASE - far.shape[1] - near.shape[1] - n_wrapped
    unused = jnp.broadcast_to(tab[:, 0:1], (A_HEADS, n_unused))
    wrapped = jnp.broadcast_to(tab[:, 2 * REL_CLIP:], (A_HEADS, n_wrapped))
    return jnp.concatenate([far, near, unused, wrapped], axis=1)


def _attn_kernel(q_ref, k_ref, v_ref, base_ref, o_ref, kext_ref, vext_ref, bias_ref):
    i = pl.program_id(1)
    half = A_HEAD_DIM

    @pl.when((pl.program_id(0) == 0) & (i == 0))
    def _():
        r = lax.broadcasted_iota(jnp.int32, (ATT_QB, ATT_KB), 0)
        c = lax.broadcasted_iota(jnp.int32, (ATT_QB, ATT_KB), 1)
        q_chunk = r // CHUNK
        k_chunk = c // CHUNK - A_LEFT // CHUNK
        band = (k_chunk <= q_chunk) & (k_chunk >= q_chunk - A_LEFT // CHUNK)
        for h in range(A_HEADS):
            rows = jnp.broadcast_to(base_ref[h:h + 1, :], (ATT_QB, ATT_BASE))
            skew = pltpu.roll(rows, 0, 1, stride=1, stride_axis=0)[:, 0:ATT_KB]
            for blk in range(ATT_BIAS_VARIANTS):
                in_seq = c >= A_LEFT - blk * ATT_QB
                bias_ref[blk * A_HEADS + h] = jnp.where(band & in_seq, skew, NEG_INF)

    @pl.when(i == 0)
    def _():
        kext_ref[0:A_LEFT, :] = jnp.zeros((A_LEFT, 2 * MIX), BF16)
        vext_ref[0:A_LEFT, :] = jnp.zeros((A_LEFT, 2 * MIX), BF16)
        t = k_ref.shape[1]
        for h in range(A_HEADS):
            src = slice(h * half, (h + 1) * half)
            kext_ref[A_LEFT:, 2 * h * half:(2 * h + 1) * half] = k_ref[0, :, src]
            kext_ref[A_LEFT:, (2 * h + 1) * half:(2 * h + 2) * half] = jnp.zeros((t, half), BF16)
            vext_ref[A_LEFT:, 2 * h * half:(2 * h + 1) * half] = v_ref[0, :, src]
            vext_ref[A_LEFT:, (2 * h + 1) * half:(2 * h + 2) * half] = jnp.ones((t, half), BF16)

    q0 = pl.multiple_of(i * ATT_QB, ATT_QB)
    blk = jnp.minimum(i, ATT_BIAS_VARIANTS - 1)
    q = q_ref[0] * (A_HEAD_DIM ** -0.5)
    lo = lax.broadcasted_iota(jnp.int32, (ATT_QB, 2 * half), 1) < half
    def scores(h):
        qh = q[:, h * half:(h + 1) * half]
        kh = kext_ref[pl.ds(q0, ATT_KB), 2 * h * half:(2 * h + 1) * half]
        return _dot_nt(qh, kh) + bias_ref[blk * A_HEADS + h]

    s_next = [scores(0), scores(1)]
    for hp in range(A_HEADS // 2):
        s_cur = s_next
        if hp + 1 < A_HEADS // 2:
            s_next = [scores(2 * hp + 2), scores(2 * hp + 3)]
        ext = []
        for j, s in enumerate(s_cur):
            h = 2 * hp + j
            vh = vext_ref[pl.ds(q0, ATT_KB), 2 * h * half:(2 * h + 2) * half]
            m = jnp.max(s, axis=-1, keepdims=True)
            p = jnp.exp(s - m)
            ext.append(_dot(p.astype(BF16), vh))
        swap_a = pltpu.roll(ext[0], half, 1)
        swap_b = pltpu.roll(ext[1], half, 1)
        out = jnp.where(lo, ext[0], swap_b) / jnp.where(lo, swap_a, ext[1])
        o_ref[0, :, 2 * hp * half:(2 * hp + 2) * half] = out.astype(o_ref.dtype)


def _attention(h3, rel_bias):
    bsz, t, _ = h3.shape
    return pl.pallas_call(
        _attn_kernel,
        grid=(bsz, t // ATT_QB),
        in_specs=[
            pl.BlockSpec((1, ATT_QB, MIX), lambda b, i: (b, i, COL_QA)),
            pl.BlockSpec((1, t, MIX), lambda b, i: (b, 0, COL_KA)),
            pl.BlockSpec((1, t, MIX), lambda b, i: (b, 0, COL_VA)),
            pl.BlockSpec((A_HEADS, ATT_BASE), lambda b, i: (0, 0)),
        ],
        out_specs=pl.BlockSpec((1, ATT_QB, MIX), lambda b, i: (b, i, 0)),
        out_shape=jax.ShapeDtypeStruct((bsz, t, MIX), BF16),
        scratch_shapes=[
            pltpu.VMEM((A_LEFT + t, 2 * MIX), BF16),
            pltpu.VMEM((A_LEFT + t, 2 * MIX), BF16),
            pltpu.VMEM((ATT_BIAS_VARIANTS * A_HEADS, ATT_QB, ATT_KB), F32),
        ],
        compiler_params=pltpu.CompilerParams(
            dimension_semantics=("arbitrary", "arbitrary"), vmem_limit_bytes=VMEM_LIMIT),
        name="attention",
    )(h3, h3, h3, _attn_bias_base(rel_bias))


CONV_TB = 256
CONV_HALO = 32
CONV_RS = 32


def _bconv_kernel(a_ref, g_ref, w_ref, cb_ref, lg_ref, lb_ref, o_ref, hbuf_ref):
    @pl.when(pl.program_id(1) == 0)
    def _():
        hbuf_ref[0:CONV_HALO, :] = jnp.zeros((CONV_HALO, MIX), F32)

    hbuf_ref[CONV_HALO:, :] = a_ref[0].astype(F32) * _sigmoid(g_ref[0].astype(F32))
    first = CONV_HALO - (B_KERNEL - 1)
    win = CONV_RS + CONV_HALO
    for r in range(CONV_TB // CONV_RS):
        accs = []
        for lt in range(MIX // LANES):
            lanes = slice(lt * LANES, (lt + 1) * LANES)
            x = hbuf_ref[r * CONV_RS:r * CONV_RS + win, lanes]
            acc = jnp.zeros((CONV_RS, LANES), F32)
            for s in range(SUBLANES):
                xs = x if s == 0 else pltpu.roll(x, win - s, 0)
                for m in range(win // SUBLANES):
                    k = SUBLANES * m + s - first
                    if 0 <= k < B_KERNEL and SUBLANES * m + s + CONV_RS <= win:
                        acc = acc + w_ref[k:k + 1, lanes] * xs[SUBLANES * m:SUBLANES * m + CONV_RS]
            accs.append(acc)
        acc = jnp.concatenate(accs, axis=1)
        y = _layer_norm(acc + cb_ref[...], lg_ref[...], lb_ref[...])
        o_ref[0, r * CONV_RS:(r + 1) * CONV_RS, :] = (y * _sigmoid(y)).astype(o_ref.dtype)
    hbuf_ref[0:CONV_HALO, :] = hbuf_ref[CONV_TB:CONV_TB + CONV_HALO, :]


def _conformer_conv(h3, conv_w, conv_b, ln_g, ln_b):
    bsz, t, _ = h3.shape
    w_pad = jnp.zeros((CONV_HALO, MIX), F32).at[:B_KERNEL].set(conv_w)
    row = lambda v: v.reshape(1, MIX).astype(F32)
    const = lambda shape: pl.BlockSpec(shape, lambda b, i: (0, 0))
    return pl.pallas_call(
        _bconv_kernel,
        grid=(bsz, t // CONV_TB),
        in_specs=[
            pl.BlockSpec((1, CONV_TB, MIX), lambda b, i: (b, i, COL_UB_A)),
            pl.BlockSpec((1, CONV_TB, MIX), lambda b, i: (b, i, COL_UB_G)),
            const((CONV_HALO, MIX)), const((1, MIX)), const((1, MIX)), const((1, MIX)),
        ],
        out_specs=pl.BlockSpec((1, CONV_TB, MIX), lambda b, i: (b, i, 0)),
        out_shape=jax.ShapeDtypeStruct((bsz, t, MIX), BF16),
        scratch_shapes=[pltpu.VMEM((CONV_HALO + CONV_TB, MIX), F32)],
        compiler_params=pltpu.CompilerParams(
            dimension_semantics=("parallel", "arbitrary"), vmem_limit_bytes=VMEM_LIMIT),
        name="conformer_conv",
    )(h3, h3, w_pad, row(conv_b), row(ln_g), row(ln_b))


DN_HALO = 16
DN_CPS = 4
DN_ROWS = DN_CPS * CHUNK
assert DN_ROWS == C_STACK


def _stack_heads(x):
    return jnp.concatenate(
        [x[:, h * C_HEAD_DIM:(h + 1) * C_HEAD_DIM] for h in range(C_HEADS)], axis=0)


def _deltanet_kernel(q_ref, k_ref, v_ref, z_ref, ba_ref, at_ref, cw_ref, prow_ref, prow_t_ref,
                     ng_ref, o_ref, xbuf_ref, state_ref):
    @pl.when(pl.program_id(1) == 0)
    def _():
        xbuf_ref[0:DN_HALO, :] = jnp.zeros((DN_HALO, 3 * MIX), BF16)
        state_ref[...] = jnp.zeros(state_ref.shape, F32)

    xbuf_ref[DN_HALO:, 0:MIX] = q_ref[0]
    xbuf_ref[DN_HALO:, MIX:2 * MIX] = k_ref[0]
    xbuf_ref[DN_HALO:, 2 * MIX:] = v_ref[0]
    sr = lax.broadcasted_iota(jnp.int32, (C_CONV * CHUNK, DN_HALO + CHUNK), 0)
    sc = lax.broadcasted_iota(jnp.int32, (C_CONV * CHUNK, DN_HALO + CHUNK), 1)
    shift_taps = (sc == sr % CHUNK + sr // CHUNK + (DN_HALO - (C_CONV - 1))).astype(BF16)

    rr = lax.broadcasted_iota(jnp.int32, (C_STACK, C_STACK), 0)
    cc = lax.broadcasted_iota(jnp.int32, (C_STACK, C_STACK), 1)
    same_block = (rr // CHUNK) == (cc // CHUNK)
    lower = same_block & (cc <= rr)
    strict = same_block & (cc < rr)
    lower_f = lower.astype(F32)
    upper_f = (same_block & (rr <= cc)).astype(F32)
    eye = (rr == cc).astype(F32)

    ba = ba_ref[0]
    beta_all = _sigmoid(ba)
    g_all = -jnp.exp(prow_ref[0:1, :]) * _softplus(ba + prow_ref[1:2, :])
    gc_all = jnp.dot(lower_f, g_all, precision=lax.Precision.HIGHEST, preferred_element_type=F32)
    g_rows = -jnp.exp(prow_t_ref[0:1, :]) * _softplus(at_ref[0, 0] + prow_t_ref[1:2, :])
    g_rows = jnp.concatenate([g_rows, jnp.zeros((SUBLANES - DN_CPS, C_STACK), F32)], axis=0)
    gc_rows = jnp.dot(g_rows, upper_f, precision=lax.Precision.HIGHEST, preferred_element_type=F32)

    def l2n(x):
        return x * lax.rsqrt(jnp.sum(x * x, axis=-1, keepdims=True) + RMS_EPS)

    def stack_col(tile, lane0, width):
        return jnp.concatenate(
            [jnp.broadcast_to(tile[:, lane0 + h:lane0 + h + 1], (CHUNK, width)) for h in range(C_HEADS)],
            axis=0)

    chunks = range(DN_CPS)
    qs, ks, vs, ks_bf = [], [], [], []
    for c in chunks:
        taps = _dot(shift_taps, xbuf_ref[c * CHUNK:(c + 1) * CHUNK + DN_HALO, :])
        acc = cw_ref[0:1, :] * taps[0:CHUNK]
        for k in range(1, C_CONV):
            acc = acc + cw_ref[k:k + 1, :] * taps[k * CHUNK:(k + 1) * CHUNK]
        qkv = acc * _sigmoid(acc)
        qs.append(l2n(_stack_heads(qkv[:, 0:MIX])) * (C_HEAD_DIM ** -0.5))
        ks.append(l2n(_stack_heads(qkv[:, MIX:2 * MIX])))
        vs.append(_stack_heads(qkv[:, 2 * MIX:]))
        ks_bf.append(ks[c].astype(BF16))
    xbuf_ref[0:DN_HALO, :] = xbuf_ref[DN_ROWS:DN_ROWS + DN_HALO, :]

    beta_b, gc_b128, decay, kb = [], [], [], []
    for c in chunks:
        tok = slice(c * CHUNK, (c + 1) * CHUNK)
        beta_b.append(stack_col(beta_all[tok], 0, C_HEAD_DIM))
        gc_b = stack_col(gc_all[tok], C_HEADS, C_STACK)
        gc_b128.append(gc_b[:, 0:C_HEAD_DIM])
        decay.append(jnp.where(lower, jnp.exp(jnp.where(lower, gc_b - gc_rows[c:c + 1, :], 0.0)), 0.0))
        kb.append(ks[c] * beta_b[c])
    lmat = [jnp.where(strict, _dot_nt(kb[c].astype(BF16), ks_bf[c]) * decay[c], 0.0) for c in chunks]
    tmat = [eye - lmat[c] for c in chunks]
    pw = lmat
    for _ in range(5):
        pw_bf = [pw[c].astype(BF16) for c in chunks]
        pw = [_dot(pw_bf[c], pw_bf[c]) for c in chunks]
        tmat = [tmat[c] + _dot(tmat[c].astype(BF16), pw[c].astype(BF16)) for c in chunks]
    exp_gc = [jnp.exp(gc_b128[c]) for c in chunks]
    uw = [_dot(tmat[c].astype(BF16),
               jnp.concatenate([vs[c] * beta_b[c], kb[c] * exp_gc[c]], axis=1).astype(BF16)) for c in chunks]
    a_intra = [jnp.where(lower, _dot_nt(qs[c].astype(BF16), ks_bf[c]) * decay[c], 0.0).astype(BF16)
               for c in chunks]
    local = [dict(u=uw[c][:, 0:C_HEAD_DIM], w=uw[c][:, C_HEAD_DIM:], qg=qs[c] * exp_gc[c], ks=ks[c],
                  gc=gc_b128[c], a=a_intra[c]) for c in chunks]

    ng = ng_ref[...]
    state = [state_ref[h] for h in range(C_HEADS)]
    for c, lc in enumerate(local):
        v_new, o_inter = [], []
        for h in range(C_HEADS):
            rows = slice(h * CHUNK, (h + 1) * CHUNK)
            lhs = jnp.concatenate([lc["w"][rows], lc["qg"][rows]], axis=0).astype(BF16)
            r = _dot(lhs, state[h].astype(BF16))
            v_new.append(lc["u"][rows] - r[0:CHUNK])
            o_inter.append(r[CHUNK:])
        v_new_bf = jnp.concatenate(v_new, axis=0).astype(BF16)
        o_s = jnp.concatenate(o_inter, axis=0) + _dot(lc["a"], v_new_bf)
        for h in range(C_HEADS):
            rows = slice(h * CHUNK, (h + 1) * CHUNK)
            cols = slice(h * C_HEAD_DIM, (h + 1) * C_HEAD_DIM)
            g_last = lc["gc"][(h + 1) * CHUNK - 1:(h + 1) * CHUNK, :]
            k_dec = lc["ks"][rows] * jnp.exp(g_last - lc["gc"][rows])
            upd = lax.dot_general(k_dec.astype(BF16), v_new_bf[rows], (((0,), (0,)), ((), ())),
                                  preferred_element_type=F32)
            state[h] = state[h] * jnp.exp(g_last) + upd
            o = o_s[rows]
            o = o * lax.rsqrt(jnp.mean(o * o, axis=-1, keepdims=True) + RMS_EPS) * ng
            z = z_ref[0, c * CHUNK:(c + 1) * CHUNK, cols].astype(F32)
            o_ref[0, c * CHUNK:(c + 1) * CHUNK, cols] = (o * (z * _sigmoid(z))).astype(o_ref.dtype)
    for h in range(C_HEADS):
        state_ref[h] = state[h]


def _deltanet(h3, ba3, conv_w, a_log, dt_bias, norm_g):
    bsz, t, _ = h3.shape
    nc = t // CHUNK
    a_t = ba3[:, :, C_HEADS:2 * C_HEADS].reshape(bsz, nc, CHUNK, C_HEADS)
    a_t = jnp.swapaxes(a_t, 2, 3).reshape(bsz, nc // DN_CPS, DN_CPS, C_STACK)
    lane_params = lambda v: jnp.zeros((BA_PAD,), F32).at[C_HEADS:2 * C_HEADS].set(v.astype(F32))
    prow = jnp.stack([lane_params(a_log), lane_params(dt_bias)])
    prow_t = jnp.stack([jnp.repeat(a_log.astype(F32), CHUNK), jnp.repeat(dt_bias.astype(F32), CHUNK)])
    col = lambda c: pl.BlockSpec((1, DN_ROWS, MIX), lambda b, n: (b, n, c))
    const = lambda shape: pl.BlockSpec(shape, lambda b, n: (0, 0))
    return pl.pallas_call(
        _deltanet_kernel,
        grid=(bsz, nc // DN_CPS),
        in_specs=[
            col(COL_QC), col(COL_KC), col(COL_VC), col(COL_ZC),
            pl.BlockSpec((1, DN_ROWS, BA_PAD), lambda b, n: (b, n, 0)),
            pl.BlockSpec((1, 1, DN_CPS, C_STACK), lambda b, n: (b, n, 0, 0)),
            const((C_CONV, 3 * MIX)), const((2, BA_PAD)), const((2, C_STACK)), const((1, C_HEAD_DIM)),
        ],
        out_specs=pl.BlockSpec((1, DN_ROWS, MIX), lambda b, n: (b, n, 0)),
        out_shape=jax.ShapeDtypeStruct((bsz, t, MIX), BF16),
        scratch_shapes=[
            pltpu.VMEM((DN_HALO + DN_ROWS, 3 * MIX), BF16),
            pltpu.VMEM((C_HEADS, C_HEAD_DIM, C_HEAD_DIM), F32),
        ],
        compiler_params=pltpu.CompilerParams(
            dimension_semantics=("parallel", "arbitrary"), vmem_limit_bytes=VMEM_LIMIT),
        name="deltanet",
    )(h3, h3, h3, h3, ba3, a_t, conv_w.astype(F32), prow, prow_t,
      norm_g.reshape(1, C_HEAD_DIM).astype(F32))


MERGE_TM = 512


def _merge_kernel(x_ref, gates_ref, ya_ref, yb_ref, yc_ref, wb_ref, wo_ref, g_ref, b_ref, o_ref):
    mix = jnp.zeros((MERGE_TM, D_MODEL), F32)
    for br, y_ref in enumerate((ya_ref, yb_ref, yc_ref)):
        gate = gates_ref[:, br * D_MODEL:(br + 1) * D_MODEL].astype(F32)
        mix = mix + _sigmoid(gate) * _dot(y_ref[...], wb_ref[br])
    z = ALPHA * x_ref[...] + _dot(mix.astype(BF16), wo_ref[...])
    o_ref[...] = _layer_norm(z, g_ref[...], b_ref[...])


def _merge(x2, h2, ya, yb, yc, w_branch, w_out, ln_g, ln_b, layer):
    n = x2.shape[0]
    tok = lambda width, c=0: pl.BlockSpec((MERGE_TM, width), lambda i: (i, c))
    return pl.pallas_call(
        _merge_kernel,
        grid=(n // MERGE_TM,),
        in_specs=[
            tok(D_MODEL), tok(3 * D_MODEL, COL_GATES), tok(MIX), tok(MIX), tok(MIX),
            pl.BlockSpec((None, 3, MIX, D_MODEL), lambda i: (layer, 0, 0, 0)),
            pl.BlockSpec((None, D_MODEL, D_MODEL), lambda i: (layer, 0, 0)),
            pl.BlockSpec((1, D_MODEL), lambda i: (0, 0)),
            pl.BlockSpec((1, D_MODEL), lambda i: (0, 0)),
        ],
        out_specs=tok(D_MODEL),
        out_shape=jax.ShapeDtypeStruct((n, D_MODEL), F32),
        compiler_params=pltpu.CompilerParams(
            dimension_semantics=("parallel",), vmem_limit_bytes=VMEM_LIMIT),
        name="merge_ln1",
    )(x2, h2, ya, yb, yc, w_branch, w_out, ln_g.reshape(1, D_MODEL), ln_b.reshape(1, D_MODEL))


MLP_TM = 512
MLP_FC = 1024


def _mlp_kernel(x_ref, p_ref, wu_ref, wd_ref, wg_ref, wp_ref, g_ref, b_ref, o_ref):
    x = x_ref[...]
    xb = x.astype(BF16)
    acc = _sigmoid(_dot(xb, wg_ref[...])) * _dot(p_ref[...].astype(BF16), wp_ref[...])
    for c in range(D_FF // MLP_FC):
        hid = jnp.maximum(_dot(xb, wu_ref[:, c * MLP_FC:(c + 1) * MLP_FC]), 0.0)
        acc = acc + _dot((hid * hid).astype(BF16), wd_ref[c * MLP_FC:(c + 1) * MLP_FC, :])
    o_ref[...] = _layer_norm(ALPHA * x + acc, g_ref[...], b_ref[...])


def _mlp(x2, p3, w_up, w_down, w_pg, w_pp, ln_g, ln_b, layer):
    n = x2.shape[0]
    once = lambda shape: pl.BlockSpec(shape, lambda i: (0, 0), pipeline_mode=pl.Buffered(1))
    weight = lambda rows, cols: pl.BlockSpec((None, rows, cols), lambda i: (layer, 0, 0),
                                             pipeline_mode=pl.Buffered(1))
    return pl.pallas_call(
        _mlp_kernel,
        grid=(n // MLP_TM,),
        in_specs=[
            pl.BlockSpec((MLP_TM, D_MODEL), lambda i: (i, 0)),
            pl.BlockSpec((None, MLP_TM, PLE_DIM), lambda i: (layer, i, 0)),
            weight(D_MODEL, D_FF), weight(D_FF, D_MODEL), weight(D_MODEL, D_MODEL),
            weight(PLE_DIM, D_MODEL), once((1, D_MODEL)), once((1, D_MODEL)),
        ],
        out_specs=pl.BlockSpec((MLP_TM, D_MODEL), lambda i: (i, 0)),
        out_shape=jax.ShapeDtypeStruct((n, D_MODEL), F32),
        compiler_params=pltpu.CompilerParams(
            dimension_semantics=("parallel",), vmem_limit_bytes=VMEM_LIMIT),
        name="mlp_ln2",
    )(x2, p3, w_up, w_down, w_pg, w_pp, ln_g.reshape(1, D_MODEL), ln_b.reshape(1, D_MODEL))


def _split_w_in(w):
    o = 0
    parts = {}
    for name, width in (("a", 3 * MIX), ("ub", 2 * MIX), ("c", 4 * MIX), ("ba", 2 * C_HEADS),
                        ("gates", 3 * D_MODEL)):
        parts[name] = w[:, :, o:o + width]
        o += width
    w_main = jnp.concatenate([parts["gates"], parts["a"], parts["ub"], parts["c"]], axis=2).astype(BF16)
    w_ba = jnp.pad(parts["ba"], ((0, 0), (0, 0), (0, BA_PAD - 2 * C_HEADS))).astype(BF16)
    return w_main, w_ba


def kernel(x, p, w_in, rel_bias, conv_w, conv_bias, conv_ln_g, conv_ln_b, dn_conv_w, dn_a_log,
           dn_dt_bias, dn_norm_g, w_branch, w_out, ln1_g, ln1_b, w_up, w_down, w_pe_gate,
           w_pe_proj, ln2_g, ln2_b):
    bsz, t, d = x.shape
    n = bsz * t
    x2 = x.reshape(n, d)
    p3 = p.reshape(DEPTH, n, PLE_DIM)
    w_main, w_ba = _split_w_in(w_in)
    w_branch, w_out, w_up, w_down, w_pe_gate, w_pe_proj = (
        w.astype(BF16) for w in (w_branch, w_out, w_up, w_down, w_pe_gate, w_pe_proj))
    for i in range(DEPTH):
        h2, ba2 = _in_proj(x2, w_main, w_ba, i)
        h3 = h2.reshape(bsz, t, H_WIDTH)
        ya = _attention(h3, rel_bias[i])
        yb = _conformer_conv(h3, conv_w[i], conv_bias[i], conv_ln_g[i], conv_ln_b[i])
        yc = _deltanet(h3, ba2.reshape(bsz, t, BA_PAD), dn_conv_w[i], dn_a_log[i], dn_dt_bias[i],
                       dn_norm_g[i])
        x2 = _merge(x2, h2, ya.reshape(n, MIX), yb.reshape(n, MIX), yc.reshape(n, MIX),
                    w_branch, w_out, ln1_g[i], ln1_b[i], i)
        x2 = _mlp(x2, p3, w_up, w_down, w_pe_gate, w_pe_proj, ln2_g[i], ln2_b[i], i)
    return x2.reshape(bsz, t, d)
```
